```python
import jax, jax.numpy as jnp
from jax import lax
import numpy as np

D_MODEL = 1024
BATCH = 8
SEQ = 2048
DEPTH = 2

D_FF = 2816
CHUNK = 128
A_HEADS = 4
A_HEAD_DIM = 128
D_A = A_HEADS * A_HEAD_DIM
B_GROUPS = 8
B_GROUP_DIM = 64
D_B = B_GROUPS * B_GROUP_DIM
D_MIX = D_A + D_B
D_IN_AB = 2 * D_A + 3 * D_B
CONV_W = 3
POOL_WINDOWS = (2, 4, 8, 16)
POOL_GROUPS = len(POOL_WINDOWS)
POOL_GROUP_DIM = D_MODEL // POOL_GROUPS
N_SUB = 3
N_EVEN = (DEPTH + 1) // 2
N_ODD = DEPTH // 2
EPS = 1e-6

kernel_name = "hybrid_gmlp_shortconv_pool_macaron_adaln"


def rmsnorm(x, g):
    xf = x.astype(jnp.float32)
    y = xf * lax.rsqrt(jnp.mean(xf * xf, axis=-1, keepdims=True) + EPS)
    return (y * g.astype(jnp.float32)).astype(x.dtype)


def layernorm(x, g):
    xf = x.astype(jnp.float32)
    mu = jnp.mean(xf, axis=-1, keepdims=True)
    var = jnp.mean(jnp.square(xf - mu), axis=-1, keepdims=True)
    y = (xf - mu) * lax.rsqrt(var + EPS)
    return (y * g.astype(jnp.float32)).astype(x.dtype)


def modulate(x, g, mod):
    shift, scale, gate = jnp.split(mod, 3, axis=-1)
    h = rmsnorm(x, g) * (1.0 + scale[:, None, :]) + shift[:, None, :]
    return h, gate[:, None, :]


def swiglu(h, w_in, w_out):
    gu = h @ w_in
    g, u = jnp.split(gu, 2, axis=-1)
    return (jax.nn.silu(g) * u) @ w_out


def spatial_gating(u, v, norm_v, w_s, b_s):
    bsz, s, _ = v.shape
    n_chunks = s // CHUNK
    v = layernorm(v, norm_v)
    vc = v.reshape(bsz, n_chunks, CHUNK, A_HEADS, A_HEAD_DIM)
    mask = jnp.tril(jnp.ones((CHUNK, CHUNK), dtype=w_s.dtype))
    z = jnp.einsum('hts,bnshd->bnthd', w_s * mask[None], vc)
    z = z + jnp.transpose(b_s)[None, None, :, :, None]
    return u * z.reshape(bsz, s, D_A)


def causal_short_conv(x, w):
    s = x.shape[1]
    xp = jnp.pad(x, ((0, 0), (CONV_W - 1, 0), (0, 0)))
    y = w[0] * xp[:, 0:s]
    for k in range(1, CONV_W):
        y = y + w[k] * xp[:, k:k + s]
    return y


def mixer_ab(h, w_in, norm_v, w_s, b_s, conv_w, w_out):
    proj = h @ w_in
    u, v, bg, cg, xb = jnp.split(
        proj, [D_A, 2 * D_A, 2 * D_A + D_B, 2 * D_A + 2 * D_B], axis=-1)
    y_a = spatial_gating(jax.nn.gelu(u), jax.nn.gelu(v), norm_v, w_s, b_s)
    y_b = bg * causal_short_conv(cg * xb, conv_w)
    return jnp.concatenate([y_a, y_b], axis=-1) @ w_out


def mixer_pool(h, w_grp, scale):
    s = h.shape[1]
    cum = jnp.cumsum(h.astype(jnp.float32), axis=1)
    t = jnp.arange(s)
    outs = []
    for i, w in enumerate(POOL_WINDOWS):
        sl = slice(i * POOL_GROUP_DIM, (i + 1) * POOL_GROUP_DIM)
        cg = cum[..., sl]
        prev = jnp.pad(cg, ((0, 0), (w, 0), (0, 0)))[:, :s]
        cnt = jnp.minimum(t + 1, w).astype(jnp.float32)[None, :, None]
        p = ((cg - prev) / cnt).astype(h.dtype) - h[..., sl]
        outs.append(p @ w_grp[i])
    return jnp.concatenate(outs, axis=-1) * scale


def setup_inputs(seed: int = 0) -> dict:
    key = jax.random.key(seed)
    ks = jax.random.split(key, 20)
    f32 = jnp.float32
    nrm = lambda k, shape, s: (jax.random.normal(k, shape, f32) * s)
    x = jax.random.normal(ks[0], (BATCH, SEQ, D_MODEL), f32)
    c = jax.random.normal(ks[1], (BATCH, D_MODEL), f32)
    norm_g = 1.0 + nrm(ks[2], (DEPTH, N_SUB, D_MODEL), 0.02)
    w_mod = nrm(ks[3], (DEPTH, D_MODEL, N_SUB * 3 * D_MODEL), 0.5 * D_MODEL ** -0.5)
    b_mod = nrm(ks[4], (DEPTH, N_SUB * 3 * D_MODEL), 0.01)
    w_ffn_in = nrm(ks[5], (DEPTH, 2, D_MODEL, 2 * D_FF), D_MODEL ** -0.5)
    w_ffn_out = nrm(ks[6], (DEPTH, 2, D_FF, D_MODEL), D_FF ** -0.5)
    ab_w_in = nrm(ks[7], (N_EVEN, D_MODEL, D_IN_AB), D_MODEL ** -0.5)
    ab_norm_v = 1.0 + nrm(ks[8], (N_EVEN, D_A), 0.02)
    ab_w_s = nrm(ks[9], (N_EVEN, A_HEADS, CHUNK, CHUNK), CHUNK ** -0.5)
    ab_b_s = 1.0 + nrm(ks[10], (N_EVEN, A_HEADS, CHUNK), 0.02)
    ab_conv_w = nrm(ks[11], (N_EVEN, CONV_W, D_B), CONV_W ** -0.5)
    ab_w_out = nrm(ks[12], (N_EVEN, D_MIX, D_MODEL), D_MIX ** -0.5)
    pool_w_grp = nrm(ks[13], (N_ODD, POOL_GROUPS, POOL_GROUP_DIM, POOL_GROUP_DIM), POOL_GROUP_DIM ** -0.5)
    pool_scale = 1.0 + nrm(ks[14], (N_ODD, D_MODEL), 0.1)
    final_g = 1.0 + nrm(ks[15], (D_MODEL,), 0.02)
    return {"x": x, "c": c, "norm_g": norm_g, "w_mod": w_mod, "b_mod": b_mod,
            "w_ffn_in": w_ffn_in, "w_ffn_out": w_ffn_out,
            "ab_w_in": ab_w_in, "ab_norm_v": ab_norm_v, "ab_w_s": ab_w_s, "ab_b_s": ab_b_s,
            "ab_conv_w": ab_conv_w, "ab_w_out": ab_w_out,
            "pool_w_grp": pool_w_grp, "pool_scale": pool_scale, "final_g": final_g}


def reference(x, c, norm_g, w_mod, b_mod, w_ffn_in, w_ffn_out,
              ab_w_in, ab_norm_v, ab_w_s, ab_b_s, ab_conv_w, ab_w_out,
              pool_w_grp, pool_scale, final_g):
    c_act = jax.nn.silu(c)
    for l in range(DEPTH):
        mod = c_act @ w_mod[l] + b_mod[l]
        mod_f1, mod_mx, mod_f2 = jnp.split(mod, N_SUB, axis=-1)
        h, gate = modulate(x, norm_g[l, 0], mod_f1)
        x = x + 0.5 * gate * swiglu(h, w_ffn_in[l, 0], w_ffn_out[l, 0])
        h, gate = modulate(x, norm_g[l, 1], mod_mx)
        if l % 2 == 0:
            j = l // 2
            y = mixer_ab(h, ab_w_in[j], ab_norm_v[j], ab_w_s[j], ab_b_s[j],
                         ab_conv_w[j], ab_w_out[j])
        else:
            j = l // 2
            y = mixer_pool(h, pool_w_grp[j], pool_scale[j])
        x = x + gate * y
        h, gate = modulate(x, norm_g[l, 2], mod_f2)
        x = x + 0.5 * gate * swiglu(h, w_ffn_in[l, 1], w_ffn_out[l, 1])
    return rmsnorm(x, final_g)
```

```python
import functools
import math

import jax
import jax.numpy as jnp
from jax import lax
from jax.experimental import pallas as pl
from jax.experimental.pallas import tpu as pltpu

EPS = 1e-6
CHUNK = 128
A_HEADS = 4
A_HEAD_DIM = 128
D_A = A_HEADS * A_HEAD_DIM
D_B = 512
CONV_W = 3
POOL_WINDOWS = (2, 4, 8, 16)
CONV_HALO = 8
POOL_HALO = 16

TOKEN_TILE = 512
FF_CHUNK = 256
MOD_TILE = 2304
VMEM_LIMIT_BYTES = 56 * 1024 * 1024

_BF16 = jnp.bfloat16
_F32 = jnp.float32


def _dot(a, b):
    return jnp.dot(a, b, preferred_element_type=_F32)


def _modulated_norm(x, g, mod):
    y = x * lax.rsqrt(jnp.mean(x * x, axis=-1, keepdims=True) + EPS)
    return (y * g) * (1.0 + mod[1:2]) + mod[0:1]


def _gelu_tanh(x):
    c = math.sqrt(2.0 / math.pi)
    return x * (0.5 * (1.0 + jnp.tanh(c * (x + 0.044715 * (x * x * x)))))


def _mod_kernel(c_ref, w_ref, b_ref, o_ref):
    c = c_ref[...]
    c_act = (c * jax.nn.sigmoid(c)).astype(_BF16)
    o_ref[...] = _dot(c_act, w_ref[...].astype(_BF16)) + b_ref[...]


def _adaln_mod(c, w_mod, b_mod):
    depth, d, n = w_mod.shape
    bsz = c.shape[0]
    return pl.pallas_call(
        _mod_kernel,
        grid=(depth, n // MOD_TILE),
        in_specs=[
            pl.BlockSpec((bsz, d), lambda l, j: (0, 0)),
            pl.BlockSpec((None, d, MOD_TILE), lambda l, j: (l, 0, j)),
            pl.BlockSpec((None, 1, MOD_TILE), lambda l, j: (l, 0, j)),
        ],
        out_specs=pl.BlockSpec((None, bsz, MOD_TILE), lambda l, j: (l, 0, j)),
        out_shape=jax.ShapeDtypeStruct((depth, bsz, n), _F32),
        compiler_params=pltpu.CompilerParams(
            dimension_semantics=("arbitrary", "arbitrary"),
            vmem_limit_bytes=VMEM_LIMIT_BYTES),
        name="adaln_mod",
    )(c, w_mod, b_mod.reshape(depth, 1, n))


def _ffn_kernel(x_ref, mod_ref, g_ref, win_ref, wout_ref, *rest, d_ff, final):
    if final:
        fg_ref, o_ref = rest
    else:
        (o_ref,) = rest
    x = x_ref[...]
    mod = mod_ref[...]
    h = _modulated_norm(x, g_ref[...], mod).astype(_BF16)
    acc = jnp.zeros(x.shape, _F32)
    for c0 in range(0, d_ff, FF_CHUNK):
        g = _dot(h, win_ref[:, c0:c0 + FF_CHUNK])
        u = _dot(h, win_ref[:, d_ff + c0:d_ff + c0 + FF_CHUNK])
        a = (g * jax.nn.sigmoid(g) * u).astype(_BF16)
        acc = acc + _dot(a, wout_ref[c0:c0 + FF_CHUNK, :])
    y = x + (0.5 * mod[2:3]) * acc
    if final:
        y = y * lax.rsqrt(jnp.mean(y * y, axis=-1, keepdims=True) + EPS) * fg_ref[...]
    o_ref[...] = y


def _resident(shape):
    return pl.BlockSpec(shape, lambda i: (0,) * len(shape), pipeline_mode=pl.Buffered(1))


def _ffn(x, mod, g, w_in, w_out, tiles_per_batch, final_g=None):
    m, d = x.shape
    d_ff = w_out.shape[0]
    final = final_g is not None
    in_specs = [
        pl.BlockSpec((TOKEN_TILE, d), lambda i: (i, 0)),
        pl.BlockSpec((None, 3, d), lambda i: (i // tiles_per_batch, 0, 0)),
        _resident((1, d)),
        _resident((d, 2 * d_ff)),
        _resident((d_ff, d)),
    ]
    args = [x, mod, g.reshape(1, d), w_in, w_out]
    if final:
        in_specs.append(_resident((1, d)))
        args.append(final_g.reshape(1, d))
    return pl.pallas_call(
        functools.partial(_ffn_kernel, d_ff=d_ff, final=final),
        grid=(m // TOKEN_TILE,),
        in_specs=in_specs,
        out_specs=pl.BlockSpec((TOKEN_TILE, d), lambda i: (i, 0)),
        out_shape=jax.ShapeDtypeStruct((m, d), _F32),
        compiler_params=pltpu.CompilerParams(
            dimension_semantics=("arbitrary",),
            vmem_limit_bytes=VMEM_LIMIT_BYTES),
        name="ffn_final" if final else "ffn",
    )(*args)


def _mixer_ab_kernel(x_ref, mod_ref, g_ref, win_ref, nv_ref, ws_ref, bias_ref, cw_ref, wout_ref,
                     o_ref, cx_ref, y_ref, *, tiles_per_batch):
    i = pl.program_id(0)
    tm = x_ref.shape[0]
    x = x_ref[...]
    mod = mod_ref[...]
    h = _modulated_norm(x, g_ref[...], mod).astype(_BF16)

    u = _gelu_tanh(_dot(h, win_ref[:, 0:D_A]))
    v = _gelu_tanh(_dot(h, win_ref[:, D_A:2 * D_A]))
    mu = jnp.mean(v, axis=-1, keepdims=True)
    vc = v - mu
    var = jnp.mean(vc * vc, axis=-1, keepdims=True)
    vn = ((vc * lax.rsqrt(var + EPS)) * nv_ref[...]).astype(_BF16)
    row = lax.broadcasted_iota(jnp.int32, (CHUNK, CHUNK), 0)
    col = lax.broadcasted_iota(jnp.int32, (CHUNK, CHUNK), 1)
    causal = col <= row
    bias = bias_ref[...]
    for hd in range(A_HEADS):
        lanes = slice(hd * A_HEAD_DIM, (hd + 1) * A_HEAD_DIM)
        w_hd = jnp.where(causal, ws_ref[hd], 0.0).astype(_BF16)
        for ch in range(tm // CHUNK):
            rows = slice(ch * CHUNK, (ch + 1) * CHUNK)
            z = _dot(w_hd, vn[rows, lanes]) + bias[:, lanes]
            y_ref[rows, lanes] = (u[rows, lanes] * z).astype(_BF16)

    bg = _dot(h, win_ref[:, 2 * D_A:2 * D_A + D_B])
    cg = _dot(h, win_ref[:, 2 * D_A + D_B:2 * D_A + 2 * D_B])
    xb = _dot(h, win_ref[:, 2 * D_A + 2 * D_B:2 * D_A + 3 * D_B])
    cx = cg * xb

    @pl.when(i % tiles_per_batch == 0)
    def _():
        cx_ref[0:CONV_HALO, :] = jnp.zeros((CONV_HALO, D_B), _F32)

    cx_ref[CONV_HALO:CONV_HALO + tm, :] = cx
    cw = cw_ref[...]
    conv = cw[0:1] * cx_ref[CONV_HALO - 2:CONV_HALO - 2 + tm, :]
    conv = conv + cw[1:2] * cx_ref[CONV_HALO - 1:CONV_HALO - 1 + tm, :]
    conv = conv + cw[2:3] * cx
    cx_ref[0:CONV_HALO, :] = cx_ref[tm:tm + CONV_HALO, :]
    y_ref[:, D_A:D_A + D_B] = (bg * conv).astype(_BF16)

    o_ref[...] = x + mod[2:3] * _dot(y_ref[...], wout_ref[...])


def _mixer_ab(x, mod, g, w_in, norm_v, w_s, b_s, conv_w, w_out, tiles_per_batch):
    m, d = x.shape
    bias = jnp.repeat(b_s.T, A_HEAD_DIM, axis=1)
    return pl.pallas_call(
        functools.partial(_mixer_ab_kernel, tiles_per_batch=tiles_per_batch),
        grid=(m // TOKEN_TILE,),
        in_specs=[
            pl.BlockSpec((TOKEN_TILE, d), lambda i: (i, 0)),
            pl.BlockSpec((None, 3, d), lambda i: (i // tiles_per_batch, 0, 0)),
            _resident((1, d)),
            _resident(w_in.shape),
            _resident((1, D_A)),
            _resident(w_s.shape),
            _resident(bias.shape),
            _resident(conv_w.shape),
            _resident(w_out.shape),
        ],
        out_specs=pl.BlockSpec((TOKEN_TILE, d), lambda i: (i, 0)),
        out_shape=jax.ShapeDtypeStruct((m, d), _F32),
        scratch_shapes=[
            pltpu.VMEM((CONV_HALO + TOKEN_TILE, D_B), _F32),
            pltpu.VMEM((TOKEN_TILE, D_A + D_B), _BF16),
        ],
        compiler_params=pltpu.CompilerParams(
            dimension_semantics=("arbitrary",),
            vmem_limit_bytes=VMEM_LIMIT_BYTES),
        name="mixer_ab",
    )(x, mod, g.reshape(1, d), w_in, norm_v.reshape(1, D_A), w_s, bias, conv_w, w_out)


def _mixer_pool_kernel(x_ref, mod_ref, g_ref, wg_ref, sc_ref, o_ref, h_ref, *, tiles_per_batch):
    i = pl.program_id(0)
    tm = x_ref.shape[0]
    gd = wg_ref.shape[-1]
    x = x_ref[...]
    mod = mod_ref[...]
    h = _modulated_norm(x, g_ref[...], mod)

    @pl.when(i % tiles_per_batch == 0)
    def _():
        h_ref[0:POOL_HALO, :] = jnp.zeros((POOL_HALO, h_ref.shape[1]), _F32)

    h_ref[POOL_HALO:POOL_HALO + tm, :] = h
    pos = (i % tiles_per_batch) * tm + lax.broadcasted_iota(jnp.int32, (tm, 1), 0) + 1
    scale = sc_ref[...]
    for k, w in enumerate(POOL_WINDOWS):
        lanes = slice(k * gd, (k + 1) * gd)
        wsum = h_ref[POOL_HALO:POOL_HALO + tm, lanes]
        for j in range(1, w):
            wsum = wsum + h_ref[POOL_HALO - j:POOL_HALO - j + tm, lanes]
        cnt = jnp.minimum(pos, w).astype(_F32)
        p = wsum / cnt - h_ref[POOL_HALO:POOL_HALO + tm, lanes]
        y = _dot(p.astype(_BF16), wg_ref[k]) * scale[:, lanes]
        o_ref[:, lanes] = x[:, lanes] + mod[2:3, lanes] * y
    h_ref[0:POOL_HALO, :] = h_ref[tm:tm + POOL_HALO, :]


def _mixer_pool(x, mod, g, w_grp, scale, tiles_per_batch):
    m, d = x.shape
    return pl.pallas_call(
        functools.partial(_mixer_pool_kernel, tiles_per_batch=tiles_per_batch),
        grid=(m // TOKEN_TILE,),
        in_specs=[
            pl.BlockSpec((TOKEN_TILE, d), lambda i: (i, 0)),
            pl.BlockSpec((None, 3, d), lambda i: (i // tiles_per_batch, 0, 0)),
            _resident((1, d)),
            _resident(w_grp.shape),
            _resident((1, d)),
        ],
        out_specs=pl.BlockSpec((TOKEN_TILE, d), lambda i: (i, 0)),
        out_shape=jax.ShapeDtypeStruct((m, d), _F32),
        scratch_shapes=[pltpu.VMEM((POOL_HALO + TOKEN_TILE, d), _F32)],
        compiler_params=pltpu.CompilerParams(
            dimension_semantics=("arbitrary",),
            vmem_limit_bytes=VMEM_LIMIT_BYTES),
        name="mixer_pool",
    )(x, mod, g.reshape(1, d), w_grp, scale.reshape(1, d))


def kernel(x, c, norm_g, w_mod, b_mod, w_ffn_in, w_ffn_out, ab_w_in, ab_norm_v, ab_w_s, ab_b_s,
           ab_conv_w, ab_w_out, pool_w_grp, pool_scale, final_g):
    bsz, seq, d = x.shape
    depth = norm_g.shape[0]
    assert seq % TOKEN_TILE == 0 and TOKEN_TILE % CHUNK == 0
    assert w_ffn_out.shape[2] % FF_CHUNK == 0 and w_mod.shape[2] % MOD_TILE == 0
    tiles_per_batch = seq // TOKEN_TILE

    mod = _adaln_mod(c, w_mod, b_mod).reshape(depth, bsz, 9, d)
    w_ffn_in = w_ffn_in.astype(_BF16)
    w_ffn_out = w_ffn_out.astype(_BF16)
    ab_w_in = ab_w_in.astype(_BF16)
    ab_w_out = ab_w_out.astype(_BF16)
    pool_w_grp = pool_w_grp.astype(_BF16)

    xf = x.reshape(bsz * seq, d)
    for l in range(depth):
        j = l // 2
        xf = _ffn(xf, mod[l, :, 0:3], norm_g[l, 0], w_ffn_in[l, 0], w_ffn_out[l, 0], tiles_per_batch)
        if l % 2 == 0:
            xf = _mixer_ab(xf, mod[l, :, 3:6], norm_g[l, 1], ab_w_in[j], ab_norm_v[j], ab_w_s[j],
                           ab_b_s[j], ab_conv_w[j], ab_w_out[j], tiles_per_batch)
        else:
            xf = _mixer_pool(xf, mod[l, :, 3:6], norm_g[l, 1], pool_w_grp[j], pool_scale[j],
                             tiles_per_batch)
        last = l == depth - 1
        xf = _ffn(xf, mod[l, :, 6:9], norm_g[l, 2], w_ffn_in[l, 1], w_ffn_out[l, 1], tiles_per_batch,
                  final_g=final_g if last else None)
    return xf.reshape(bsz, seq, d)
```

```python
import functools
import math

import jax
import jax.numpy as jnp
from jax import lax
from jax.experimental import pallas as pl
from jax.experimental.pallas import tpu as pltpu

EPS = 1e-6
CHUNK = 128
A_HEADS = 4
A_HEAD_DIM = 128
D_A = A_HEADS * A_HEAD_DIM
D_B = 512
CONV_W = 3
POOL_WINDOWS = (2, 4, 8, 16)
CONV_HALO = 8
POOL_HALO = 16

FFN_TILE = 1024
MIX_TILE = 512
FF_CHUNK = 256
MOD_TILE = 2304
VMEM_LIMIT_BYTES = 56 * 1024 * 1024

_BF16 = jnp.bfloat16
_F32 = jnp.float32


def _dot(a, b):
    return jnp.dot(a, b, preferred_element_type=_F32)


def _modulated_norm(x, g, mod):
    y = x * lax.rsqrt(jnp.mean(x * x, axis=-1, keepdims=True) + EPS)
    return (y * g) * (1.0 + mod[1:2]) + mod[0:1]


def _gelu_tanh(x):
    c = math.sqrt(2.0 / math.pi)
    return x * (0.5 * (1.0 + jnp.tanh(c * (x + 0.044715 * (x * x * x)))))


def _mod_kernel(c_ref, w_ref, b_ref, o_ref):
    c = c_ref[...]
    c_act = (c * jax.nn.sigmoid(c)).astype(_BF16)
    o_ref[...] = _dot(c_act, w_ref[...].astype(_BF16)) + b_ref[...]


def _adaln_mod(c, w_mod, b_mod):
    depth, d, n = w_mod.shape
    bsz = c.shape[0]
    return pl.pallas_call(
        _mod_kernel,
        grid=(depth, n // MOD_TILE),
        in_specs=[
            pl.BlockSpec((bsz, d), lambda l, j: (0, 0)),
            pl.BlockSpec((None, d, MOD_TILE), lambda l, j: (l, 0, j)),
            pl.BlockSpec((None, 1, MOD_TILE), lambda l, j: (l, 0, j)),
        ],
        out_specs=pl.BlockSpec((None, bsz, MOD_TILE), lambda l, j: (l, 0, j)),
        out_shape=jax.ShapeDtypeStruct((depth, bsz, n), _F32),
        compiler_params=pltpu.CompilerParams(
            dimension_semantics=("arbitrary", "arbitrary"),
            vmem_limit_bytes=VMEM_LIMIT_BYTES),
        name="adaln_mod",
    )(c, w_mod, b_mod.reshape(depth, 1, n))


def _ffn_kernel(x_ref, mod_ref, g_ref, win_ref, wout_ref, *rest, d_ff, final):
    if final:
        fg_ref, o_ref = rest
    else:
        (o_ref,) = rest
    x = x_ref[...]
    mod = mod_ref[...]
    h = _modulated_norm(x, g_ref[...], mod).astype(_BF16)
    acc = jnp.zeros(x.shape, _F32)
    for c0 in range(0, d_ff, FF_CHUNK):
        g = _dot(h, win_ref[:, c0:c0 + FF_CHUNK])
        u = _dot(h, win_ref[:, d_ff + c0:d_ff + c0 + FF_CHUNK])
        a = (g * jax.nn.sigmoid(g) * u).astype(_BF16)
        acc = acc + _dot(a, wout_ref[c0:c0 + FF_CHUNK, :])
    y = x + (0.5 * mod[2:3]) * acc
    if final:
        y = y * lax.rsqrt(jnp.mean(y * y, axis=-1, keepdims=True) + EPS) * fg_ref[...]
    o_ref[...] = y


def _resident(shape, lead=()):
    block = (None,) * len(lead) + tuple(shape)
    index = tuple(lead) + (0,) * len(shape)
    return pl.BlockSpec(block, lambda i: index, pipeline_mode=pl.Buffered(1))


def _mod_spec(layer, sub, tiles_per_batch, d):
    return pl.BlockSpec((None, None, None, 3, d), lambda i: (layer, i // tiles_per_batch, sub, 0, 0))


def _ffn(x, mod, g, w_in, w_out, layer, which, seq, final_g=None):
    m, d = x.shape
    d_ff = w_out.shape[2]
    final = final_g is not None
    in_specs = [
        pl.BlockSpec((FFN_TILE, d), lambda i: (i, 0)),
        _mod_spec(layer, 2 * which, seq // FFN_TILE, d),
        _resident((1, d)),
        _resident((d, 2 * d_ff), lead=(layer, which)),
        _resident((d_ff, d), lead=(layer, which)),
    ]
    args = [x, mod, g.reshape(1, d), w_in, w_out]
    if final:
        in_specs.append(_resident((1, d)))
        args.append(final_g.reshape(1, d))
    return pl.pallas_call(
        functools.partial(_ffn_kernel, d_ff=d_ff, final=final),
        grid=(m // FFN_TILE,),
        in_specs=in_specs,
        out_specs=pl.BlockSpec((FFN_TILE, d), lambda i: (i, 0)),
        out_shape=jax.ShapeDtypeStruct((m, d), _F32),
        compiler_params=pltpu.CompilerParams(
            dimension_semantics=("arbitrary",),
            vmem_limit_bytes=VMEM_LIMIT_BYTES),
        name="ffn_final" if final else "ffn",
    )(*args)


def _mixer_ab_kernel(x_ref, mod_ref, g_ref, win_ref, nv_ref, ws_ref, bias_ref, cw_ref, wout_ref,
                     o_ref, cx_ref, y_ref, *, tiles_per_batch):
    i = pl.program_id(0)
    tm = x_ref.shape[0]
    x = x_ref[...]
    mod = mod_ref[...]
    h = _modulated_norm(x, g_ref[...], mod).astype(_BF16)

    u = _gelu_tanh(_dot(h, win_ref[:, 0:D_A]))
    v = _gelu_tanh(_dot(h, win_ref[:, D_A:2 * D_A]))
    mu = jnp.mean(v, axis=-1, keepdims=True)
    vc = v - mu
    var = jnp.mean(vc * vc, axis=-1, keepdims=True)
    vn = ((vc * lax.rsqrt(var + EPS)) * nv_ref[...]).astype(_BF16)
    row = lax.broadcasted_iota(jnp.int32, (CHUNK, CHUNK), 0)
    col = lax.broadcasted_iota(jnp.int32, (CHUNK, CHUNK), 1)
    causal = col <= row
    bias = bias_ref[...]
    for hd in range(A_HEADS):
        lanes = slice(hd * A_HEAD_DIM, (hd + 1) * A_HEAD_DIM)
        w_hd = jnp.where(causal, ws_ref[hd], 0.0).astype(_BF16)
        for ch in range(tm // CHUNK):
            rows = slice(ch * CHUNK, (ch + 1) * CHUNK)
            z = _dot(w_hd, vn[rows, lanes]) + bias[:, lanes]
            y_ref[rows, lanes] = (u[rows, lanes] * z).astype(_BF16)

    bg = _dot(h, win_ref[:, 2 * D_A:2 * D_A + D_B])
    cg = _dot(h, win_ref[:, 2 * D_A + D_B:2 * D_A + 2 * D_B])
    xb = _dot(h, win_ref[:, 2 * D_A + 2 * D_B:2 * D_A + 3 * D_B])
    cx = cg * xb

    @pl.when(i % tiles_per_batch == 0)
    def _():
        cx_ref[0:CONV_HALO, :] = jnp.zeros((CONV_HALO, D_B), _F32)

    cx_ref[CONV_HALO:CONV_HALO + tm, :] = cx
    cw = cw_ref[...]
    conv = cw[0:1] * cx_ref[CONV_HALO - 2:CONV_HALO - 2 + tm, :]
    conv = conv + cw[1:2] * cx_ref[CONV_HALO - 1:CONV_HALO - 1 + tm, :]
    conv = conv + cw[2:3] * cx
    cx_ref[0:CONV_HALO, :] = cx_ref[tm:tm + CONV_HALO, :]
    y_ref[:, D_A:D_A + D_B] = (bg * conv).astype(_BF16)

    o_ref[...] = x + mod[2:3] * _dot(y_ref[...], wout_ref[...])


def _mixer_ab(x, mod, g, w_in, norm_v, w_s, b_s, conv_w, w_out, layer, seq):
    m, d = x.shape
    tiles_per_batch = seq // MIX_TILE
    bias = jnp.repeat(b_s.T, A_HEAD_DIM, axis=1)
    return pl.pallas_call(
        functools.partial(_mixer_ab_kernel, tiles_per_batch=tiles_per_batch),
        grid=(m // MIX_TILE,),
        in_specs=[
            pl.BlockSpec((MIX_TILE, d), lambda i: (i, 0)),
            _mod_spec(layer, 1, tiles_per_batch, d),
            _resident((1, d)),
            _resident(w_in.shape),
            _resident((1, D_A)),
            _resident(w_s.shape),
            _resident(bias.shape),
            _resident(conv_w.shape),
            _resident(w_out.shape),
        ],
        out_specs=pl.BlockSpec((MIX_TILE, d), lambda i: (i, 0)),
        out_shape=jax.ShapeDtypeStruct((m, d), _F32),
        scratch_shapes=[
            pltpu.VMEM((CONV_HALO + MIX_TILE, D_B), _F32),
            pltpu.VMEM((MIX_TILE, D_A + D_B), _BF16),
        ],
        compiler_params=pltpu.CompilerParams(
            dimension_semantics=("arbitrary",),
            vmem_limit_bytes=VMEM_LIMIT_BYTES),
        name="mixer_ab",
    )(x, mod, g.reshape(1, d), w_in, norm_v.reshape(1, D_A), w_s, bias, conv_w, w_out)


def _mixer_pool_kernel(x_ref, mod_ref, g_ref, wg_ref, sc_ref, o_ref, h_ref, *, tiles_per_batch):
    i = pl.program_id(0)
    tm = x_ref.shape[0]
    gd = wg_ref.shape[-1]
    x = x_ref[...]
    mod = mod_ref[...]
    h = _modulated_norm(x, g_ref[...], mod)

    @pl.when(i % tiles_per_batch == 0)
    def _():
        h_ref[0:POOL_HALO, :] = jnp.zeros((POOL_HALO, h_ref.shape[1]), _F32)

    h_ref[POOL_HALO:POOL_HALO + tm, :] = h
    pos = (i % tiles_per_batch) * tm + lax.broadcasted_iota(jnp.int32, (tm, 1), 0) + 1
    scale = sc_ref[...]
    for k, w in enumerate(POOL_WINDOWS):
        lanes = slice(k * gd, (k + 1) * gd)
        wsum = h_ref[POOL_HALO:POOL_HALO + tm, lanes]
        for j in range(1, w):
            wsum = wsum + h_ref[POOL_HALO - j:POOL_HALO - j + tm, lanes]
        cnt = jnp.minimum(pos, w).astype(_F32)
        p = wsum / cnt - h_ref[POOL_HALO:POOL_HALO + tm, lanes]
        y = _dot(p.astype(_BF16), wg_ref[k]) * scale[:, lanes]
        o_ref[:, lanes] = x[:, lanes] + mod[2:3, lanes] * y
    h_ref[0:POOL_HALO, :] = h_ref[tm:tm + POOL_HALO, :]


def _mixer_pool(x, mod, g, w_grp, scale, layer, seq):
    m, d = x.shape
    tiles_per_batch = seq // MIX_TILE
    return pl.pallas_call(
        functools.partial(_mixer_pool_kernel, tiles_per_batch=tiles_per_batch),
        grid=(m // MIX_TILE,),
        in_specs=[
            pl.BlockSpec((MIX_TILE, d), lambda i: (i, 0)),
            _mod_spec(layer, 1, tiles_per_batch, d),
            _resident((1, d)),
            _resident(w_grp.shape),
            _resident((1, d)),
        ],
        out_specs=pl.BlockSpec((MIX_TILE, d), lambda i: (i, 0)),
        out_shape=jax.ShapeDtypeStruct((m, d), _F32),
        scratch_shapes=[pltpu.VMEM((POOL_HALO + MIX_TILE, d), _F32)],
        compiler_params=pltpu.CompilerParams(
            dimension_semantics=("arbitrary",),
            vmem_limit_bytes=VMEM_LIMIT_BYTES),
        name="mixer_pool",
    )(x, mod, g.reshape(1, d), w_grp, scale.reshape(1, d))


def kernel(x, c, norm_g, w_mod, b_mod, w_ffn_in, w_ffn_out, ab_w_in, ab_norm_v, ab_w_s, ab_b_s,
           ab_conv_w, ab_w_out, pool_w_grp, pool_scale, final_g):
    bsz, seq, d = x.shape
    depth = norm_g.shape[0]
    assert seq % FFN_TILE == 0 and seq % MIX_TILE == 0 and MIX_TILE % CHUNK == 0
    assert w_ffn_out.shape[2] % FF_CHUNK == 0 and w_mod.shape[2] % MOD_TILE == 0

    mod = _adaln_mod(c, w_mod, b_mod).reshape(depth, bsz, 3, 3, d)
    w_ffn_in = w_ffn_in.astype(_BF16)
    w_ffn_out = w_ffn_out.astype(_BF16)
    ab_w_in = ab_w_in.astype(_BF16)
    ab_w_out = ab_w_out.astype(_BF16)
    pool_w_grp = pool_w_grp.astype(_BF16)

    xf = x.reshape(bsz * seq, d)
    for l in range(depth):
        j = l // 2
        xf = _ffn(xf, mod, norm_g[l, 0], w_ffn_in, w_ffn_out, l, 0, seq)
        if l % 2 == 0:
            xf = _mixer_ab(xf, mod, norm_g[l, 1], ab_w_in[j], ab_norm_v[j], ab_w_s[j],
                           ab_b_s[j], ab_conv_w[j], ab_w_out[j], l, seq)
        else:
            xf = _mixer_pool(xf, mod, norm_g[l, 1], pool_w_grp[j], pool_scale[j], l, seq)
        last = l == depth - 1
        xf = _ffn(xf, mod, norm_g[l, 2], w_ffn_in, w_ffn_out, l, 1, seq,
                  final_g=final_g if last else None)
    return xf.reshape(bsz, seq, d)
```

```python
import functools
import math

import jax
import jax.numpy as jnp
from jax import lax
from jax.experimental import pallas as pl
from jax.experimental.pallas import tpu as pltpu

EPS = 1e-6
CHUNK = 128
A_HEADS = 4
A_HEAD_DIM = 128
D_A = A_HEADS * A_HEAD_DIM
D_B = 512
CONV_W = 3
POOL_WINDOWS = (2, 4, 8, 16)
CONV_HALO = 8
POOL_HALO = 16

FFN_TILE = 512
MIX_TILE = 512
FF_CHUNK = 256
AB_PIECE = 512
MOD_TILE = 2304
VMEM_LIMIT_BYTES = 56 * 1024 * 1024

_BF16 = jnp.bfloat16
_F32 = jnp.float32


def _dot(a, b):
    return jnp.dot(a, b, preferred_element_type=_F32)


def _modulated_norm(x, g, mod):
    y = x * lax.rsqrt(jnp.mean(x * x, axis=-1, keepdims=True) + EPS)
    return (y * g) * (1.0 + mod[1:2]) + mod[0:1]


def _gelu_tanh(x):
    c = math.sqrt(2.0 / math.pi)
    return x * (0.5 * (1.0 + jnp.tanh(c * (x + 0.044715 * (x * x * x)))))


def _resident(shape):
    return pl.BlockSpec(shape, lambda i: (0,) * len(shape), pipeline_mode=pl.Buffered(1))


def _mod_spec(layer, sub, tiles_per_batch, d):
    return pl.BlockSpec((None, None, None, 3, d), lambda i: (layer, i // tiles_per_batch, sub, 0, 0))


class _WeightStager:
    def __init__(self, pieces, sem):
        self.pieces = pieces
        self.sem = sem

    def _copies(self, k):
        slot = k % 2
        return [pltpu.make_async_copy(src, stage(slot), self.sem.at[slot, part])
                for part, (src, stage, _) in enumerate(self.pieces[k])]

    def _start(self, k):
        for cp in self._copies(k):
            cp.start()

    def use(self, k):
        if k == 0:
            self._start(0)
        if k + 1 < len(self.pieces):
            self._start(k + 1)
        for cp in self._copies(k):
            cp.wait()
        for _, stage, dst in self.pieces[k]:
            dst[...] = stage(k % 2)[...].astype(_BF16)


def _mod_kernel(c_ref, w_ref, b_ref, o_ref):
    c = c_ref[...]
    c_act = (c * jax.nn.sigmoid(c)).astype(_BF16)
    o_ref[...] = _dot(c_act, w_ref[...].astype(_BF16)) + b_ref[...]


def _adaln_mod(c, w_mod, b_mod):
    depth, d, n = w_mod.shape
    bsz = c.shape[0]
    return pl.pallas_call(
        _mod_kernel,
        grid=(depth, n // MOD_TILE),
        in_specs=[
            pl.BlockSpec((bsz, d), lambda l, j: (0, 0)),
            pl.BlockSpec((None, d, MOD_TILE), lambda l, j: (l, 0, j)),
            pl.BlockSpec((None, 1, MOD_TILE), lambda l, j: (l, 0, j)),
        ],
        out_specs=pl.BlockSpec((None, bsz, MOD_TILE), lambda l, j: (l, 0, j)),
        out_shape=jax.ShapeDtypeStruct((depth, bsz, n), _F32),
        compiler_params=pltpu.CompilerParams(
            dimension_semantics=("arbitrary", "arbitrary"),
            vmem_limit_bytes=VMEM_LIMIT_BYTES),
        name="adaln_mod",
    )(c, w_mod, b_mod.reshape(depth, 1, n))


def _ffn_kernel(x_ref, mod_ref, g_ref, win_hbm, wout_hbm, *rest, layer, which, d_ff, final):
    if final:
        fg_ref, o_ref, win_ref, wout_ref, stg_in_ref, stg_out_ref, sem = rest
    else:
        o_ref, win_ref, wout_ref, stg_in_ref, stg_out_ref, sem = rest
    chunks = list(range(0, d_ff, FF_CHUNK))

    def gate_cols(c0):
        return slice(c0, c0 + FF_CHUNK)

    def up_cols(c0):
        return slice(d_ff + c0, d_ff + c0 + FF_CHUNK)

    def body(stager):
        x = x_ref[...]
        mod = mod_ref[...]
        h = _modulated_norm(x, g_ref[...], mod).astype(_BF16)
        acc = jnp.zeros(x.shape, _F32)
        for k, c0 in enumerate(chunks):
            if stager is not None:
                stager.use(k)
            g = _dot(h, win_ref[:, gate_cols(c0)])
            u = _dot(h, win_ref[:, up_cols(c0)])
            a = (g * jax.nn.sigmoid(g) * u).astype(_BF16)
            acc = acc + _dot(a, wout_ref[c0:c0 + FF_CHUNK, :])
        y = x + (0.5 * mod[2:3]) * acc
        if final:
            y = y * lax.rsqrt(jnp.mean(y * y, axis=-1, keepdims=True) + EPS) * fg_ref[...]
        o_ref[...] = y

    pieces = [[
        (win_hbm.at[layer, which, :, gate_cols(c0)], lambda s: stg_in_ref.at[s, 0], win_ref.at[:, gate_cols(c0)]),
        (win_hbm.at[layer, which, :, up_cols(c0)], lambda s: stg_in_ref.at[s, 1], win_ref.at[:, up_cols(c0)]),
        (wout_hbm.at[layer, which, c0:c0 + FF_CHUNK, :], lambda s: stg_out_ref.at[s], wout_ref.at[c0:c0 + FF_CHUNK, :]),
    ] for c0 in chunks]

    first = pl.program_id(0) == 0

    @pl.when(first)
    def _():
        body(_WeightStager(pieces, sem))

    @pl.when(jnp.logical_not(first))
    def _():
        body(None)


def _ffn(x, mod, g, w_in, w_out, layer, which, seq, final_g=None):
    m, d = x.shape
    d_ff = w_out.shape[2]
    final = final_g is not None
    in_specs = [
        pl.BlockSpec((FFN_TILE, d), lambda i: (i, 0)),
        _mod_spec(layer, 2 * which, seq // FFN_TILE, d),
        _resident((1, d)),
        pl.BlockSpec(memory_space=pl.ANY),
        pl.BlockSpec(memory_space=pl.ANY),
    ]
    args = [x, mod, g.reshape(1, d), w_in, w_out]
    if final:
        in_specs.append(_resident((1, d)))
        args.append(final_g.reshape(1, d))
    return pl.pallas_call(
        functools.partial(_ffn_kernel, layer=layer, which=which, d_ff=d_ff, final=final),
        grid=(m // FFN_TILE,),
        in_specs=in_specs,
        out_specs=pl.BlockSpec((FFN_TILE, d), lambda i: (i, 0)),
        out_shape=jax.ShapeDtypeStruct((m, d), _F32),
        scratch_shapes=[
            pltpu.VMEM((d, 2 * d_ff), _BF16),
            pltpu.VMEM((d_ff, d), _BF16),
            pltpu.VMEM((2, 2, d, FF_CHUNK), _F32),
            pltpu.VMEM((2, FF_CHUNK, d), _F32),
            pltpu.SemaphoreType.DMA((2, 3)),
        ],
        compiler_params=pltpu.CompilerParams(
            dimension_semantics=("arbitrary",),
            vmem_limit_bytes=VMEM_LIMIT_BYTES),
        name="ffn_final" if final else "ffn",
    )(*args)


def _mixer_ab_kernel(x_ref, mod_ref, g_ref, win_hbm, nv_ref, ws_ref, bias_ref, cw_ref, wout_hbm,
                     o_ref, win_ref, wout_ref, stg_ref, sem, cx_ref, y_ref, *, layer_slot, tiles_per_batch):
    i = pl.program_id(0)
    tm, d = x_ref.shape
    n_in = win_ref.shape[1] // AB_PIECE

    def in_cols(k):
        return slice(k * AB_PIECE, (k + 1) * AB_PIECE)

    def body(stager):
        def w_in(k):
            if stager is not None:
                stager.use(k)
            return win_ref[:, in_cols(k)]

        x = x_ref[...]
        mod = mod_ref[...]
        h = _modulated_norm(x, g_ref[...], mod).astype(_BF16)

        u = _gelu_tanh(_dot(h, w_in(0)))
        v = _gelu_tanh(_dot(h, w_in(1)))
        mu = jnp.mean(v, axis=-1, keepdims=True)
        vc = v - mu
        var = jnp.mean(vc * vc, axis=-1, keepdims=True)
        vn = ((vc * lax.rsqrt(var + EPS)) * nv_ref[...]).astype(_BF16)
        row = lax.broadcasted_iota(jnp.int32, (CHUNK, CHUNK), 0)
        col = lax.broadcasted_iota(jnp.int32, (CHUNK, CHUNK), 1)
        causal = col <= row
        bias = bias_ref[...]
        for hd in range(A_HEADS):
            lanes = slice(hd * A_HEAD_DIM, (hd + 1) * A_HEAD_DIM)
            w_hd = jnp.where(causal, ws_ref[hd], 0.0).astype(_BF16)
            for ch in range(tm // CHUNK):
                rows = slice(ch * CHUNK, (ch + 1) * CHUNK)
                z = _dot(w_hd, vn[rows, lanes]) + bias[:, lanes]
                y_ref[rows, lanes] = (u[rows, lanes] * z).astype(_BF16)

        bg = _dot(h, w_in(2))
        cg = _dot(h, w_in(3))
        xb = _dot(h, w_in(4))
        cx = cg * xb

        @pl.when(i % tiles_per_batch == 0)
        def _():
            cx_ref[0:CONV_HALO, :] = jnp.zeros((CONV_HALO, D_B), _F32)

        cx_ref[CONV_HALO:CONV_HALO + tm, :] = cx
        cw = cw_ref[...]
        conv = cw[0:1] * cx_ref[CONV_HALO - 2:CONV_HALO - 2 + tm, :]
        conv = conv + cw[1:2] * cx_ref[CONV_HALO - 1:CONV_HALO - 1 + tm, :]
        conv = conv + cw[2:3] * cx
        cx_ref[0:CONV_HALO, :] = cx_ref[tm:tm + CONV_HALO, :]
        y_ref[:, D_A:D_A + D_B] = (bg * conv).astype(_BF16)

        if stager is not None:
            for k in range(n_in, len(stager.pieces)):
                stager.use(k)
        o_ref[...] = x + mod[2:3] * _dot(y_ref[...], wout_ref[...])

    def stage(s):
        return stg_ref.at[s]

    pieces = [[(win_hbm.at[layer_slot, :, in_cols(k)], stage, win_ref.at[:, in_cols(k)])] for k in range(n_in)]
    pieces += [[(wout_hbm.at[layer_slot, :, in_cols(k)], stage, wout_ref.at[:, in_cols(k)])]
               for k in range(d // AB_PIECE)]

    @pl.when(i == 0)
    def _():
        body(_WeightStager(pieces, sem))

    @pl.when(i != 0)
    def _():
        body(None)


def _mixer_ab(x, mod, g, w_in, norm_v, w_s, b_s, conv_w, w_out, layer, layer_slot, seq):
    m, d = x.shape
    d_mix = w_out.shape[1]
    tiles_per_batch = seq // MIX_TILE
    bias = jnp.repeat(b_s.T, A_HEAD_DIM, axis=1)
    return pl.pallas_call(
        functools.partial(_mixer_ab_kernel, layer_slot=layer_slot, tiles_per_batch=tiles_per_batch),
        grid=(m // MIX_TILE,),
        in_specs=[
            pl.BlockSpec((MIX_TILE, d), lambda i: (i, 0)),
            _mod_spec(layer, 1, tiles_per_batch, d),
            _resident((1, d)),
            pl.BlockSpec(memory_space=pl.ANY),
            _resident((1, D_A)),
            _resident(w_s.shape),
            _resident(bias.shape),
            _resident(conv_w.shape),
            pl.BlockSpec(memory_space=pl.ANY),
        ],
        out_specs=pl.BlockSpec((MIX_TILE, d), lambda i: (i, 0)),
        out_shape=jax.ShapeDtypeStruct((m, d), _F32),
        scratch_shapes=[
            pltpu.VMEM((d, w_in.shape[2]), _BF16),
            pltpu.VMEM((d_mix, d), _BF16),
            pltpu.VMEM((2, d, AB_PIECE), _F32),
            pltpu.SemaphoreType.DMA((2, 1)),
            pltpu.VMEM((CONV_HALO + MIX_TILE, D_B), _F32),
            pltpu.VMEM((MIX_TILE, D_A + D_B), _BF16),
        ],
        compiler_params=pltpu.CompilerParams(
            dimension_semantics=("arbitrary",),
            vmem_limit_bytes=VMEM_LIMIT_BYTES),
        name="mixer_ab",
    )(x, mod, g.reshape(1, d), w_in, norm_v.reshape(1, D_A), w_s, bias, conv_w, w_out)


def _mixer_pool_kernel(x_ref, mod_ref, g_ref, wg_ref, sc_ref, o_ref, h_ref, *, tiles_per_batch):
    i = pl.program_id(0)
    tm = x_ref.shape[0]
    gd = wg_ref.shape[-1]
    x = x_ref[...]
    mod = mod_ref[...]
    h = _modulated_norm(x, g_ref[...], mod)

    @pl.when(i % tiles_per_batch == 0)
    def _():
        h_ref[0:POOL_HALO, :] = jnp.zeros((POOL_HALO, h_ref.shape[1]), _F32)

    h_ref[POOL_HALO:POOL_HALO + tm, :] = h
    pos = (i % tiles_per_batch) * tm + lax.broadcasted_iota(jnp.int32, (tm, 1), 0) + 1
    scale = sc_ref[...]
    for k, w in enumerate(POOL_WINDOWS):
        lanes = slice(k * gd, (k + 1) * gd)
        wsum = h_ref[POOL_HALO:POOL_HALO + tm, lanes]
        for j in range(1, w):
            wsum = wsum + h_ref[POOL_HALO - j:POOL_HALO - j + tm, lanes]
        cnt = jnp.minimum(pos, w).astype(_F32)
        p = wsum / cnt - h_ref[POOL_HALO:POOL_HALO + tm, lanes]
        y = _dot(p.astype(_BF16), wg_ref[k].astype(_BF16)) * scale[:, lanes]
        o_ref[:, lanes] = x[:, lanes] + mod[2:3, lanes] * y
    h_ref[0:POOL_HALO, :] = h_ref[tm:tm + POOL_HALO, :]


def _mixer_pool(x, mod, g, w_grp, scale, layer, seq):
    m, d = x.shape
    tiles_per_batch = seq // MIX_TILE
    return pl.pallas_call(
        functools.partial(_mixer_pool_kernel, tiles_per_batch=tiles_per_batch),
        grid=(m // MIX_TILE,),
        in_specs=[
            pl.BlockSpec((MIX_TILE, d), lambda i: (i, 0)),
            _mod_spec(layer, 1, tiles_per_batch, d),
            _resident((1, d)),
            _resident(w_grp.shape),
            _resident((1, d)),
        ],
        out_specs=pl.BlockSpec((MIX_TILE, d), lambda i: (i, 0)),
        out_shape=jax.ShapeDtypeStruct((m, d), _F32),
        scratch_shapes=[pltpu.VMEM((POOL_HALO + MIX_TILE, d), _F32)],
        compiler_params=pltpu.CompilerParams(
            dimension_semantics=("arbitrary",),
            vmem_limit_bytes=VMEM_LIMIT_BYTES),
        name="mixer_pool",
    )(x, mod, g.reshape(1, d), w_grp, scale.reshape(1, d))


def kernel(x, c, norm_g, w_mod, b_mod, w_ffn_in, w_ffn_out, ab_w_in, ab_norm_v, ab_w_s, ab_b_s,
           ab_conv_w, ab_w_out, pool_w_grp, pool_scale, final_g):
    bsz, seq, d = x.shape
    depth = norm_g.shape[0]
    assert seq % FFN_TILE == 0 and seq % MIX_TILE == 0 and MIX_TILE % CHUNK == 0
    assert w_ffn_out.shape[2] % FF_CHUNK == 0 and w_mod.shape[2] % MOD_TILE == 0
    assert ab_w_in.shape[2] % AB_PIECE == 0 and d % AB_PIECE == 0

    mod = _adaln_mod(c, w_mod, b_mod).reshape(depth, bsz, 3, 3, d)

    xf = x.reshape(bsz * seq, d)
    for l in range(depth):
        j = l // 2
        xf = _ffn(xf, mod, norm_g[l, 0], w_ffn_in, w_ffn_out, l, 0, seq)
        if l % 2 == 0:
            xf = _mixer_ab(xf, mod, norm_g[l, 1], ab_w_in, ab_norm_v[j], ab_w_s[j], ab_b_s[j], ab_conv_w[j],
                           ab_w_out, l, j, seq)
        else:
            xf = _mixer_pool(xf, mod, norm_g[l, 1], pool_w_grp[j], pool_scale[j], l, seq)
        last = l == depth - 1
        xf = _ffn(xf, mod, norm_g[l, 2], w_ffn_in, w_ffn_out, l, 1, seq,
                  final_g=final_g if last else None)
    return xf.reshape(bsz, seq, d)
```

```python
import functools
import math

import jax
import jax.numpy as jnp
from jax import lax
from jax.experimental import pallas as pl
from jax.experimental.pallas import tpu as pltpu

EPS = 1e-6
CHUNK = 128
A_HEADS = 4
A_HEAD_DIM = 128
D_A = A_HEADS * A_HEAD_DIM
D_B = 512
CONV_W = 3
POOL_WINDOWS = (2, 4, 8, 16)
CONV_HALO = 8
POOL_HALO = 16
POOL_BLOCK = 128

FFN_TILE = 512
MIX_TILE = 512
FF_CHUNK = 256
AB_PIECE = 512
MOD_TILE = 2304
VMEM_LIMIT_BYTES = 56 * 1024 * 1024

_BF16 = jnp.bfloat16
_F32 = jnp.float32


def _dot(a, b):
    return jnp.dot(a, b, preferred_element_type=_F32)


def _modulated_norm(x, g, mod):
    y = x * lax.rsqrt(jnp.mean(x * x, axis=-1, keepdims=True) + EPS)
    return (y * g) * (1.0 + mod[1:2]) + mod[0:1]


def _gelu_tanh(x):
    c = math.sqrt(2.0 / math.pi)
    return x * (0.5 * (1.0 + jnp.tanh(c * (x + 0.044715 * (x * x * x)))))


def _resident(shape):
    return pl.BlockSpec(shape, lambda i: (0,) * len(shape), pipeline_mode=pl.Buffered(1))


def _mod_spec(layer, sub, tiles_per_batch, d):
    return pl.BlockSpec((None, None, None, 3, d), lambda i: (layer, i // tiles_per_batch, sub, 0, 0))


class _WeightStager:
    def __init__(self, pieces, sem):
        self.pieces = pieces
        self.sem = sem

    def _copies(self, k):
        slot = k % 2
        return [pltpu.make_async_copy(src, stage(slot), self.sem.at[slot, part])
                for part, (src, stage, _) in enumerate(self.pieces[k])]

    def _start(self, k):
        for cp in self._copies(k):
            cp.start()

    def use(self, k):
        if k == 0:
            self._start(0)
        if k + 1 < len(self.pieces):
            self._start(k + 1)
        for cp in self._copies(k):
            cp.wait()
        for _, stage, dst in self.pieces[k]:
            dst[...] = stage(k % 2)[...].astype(_BF16)


def _mod_kernel(c_ref, w_ref, b_ref, o_ref):
    c = c_ref[...]
    c_act = (c * jax.nn.sigmoid(c)).astype(_BF16)
    o_ref[...] = _dot(c_act, w_ref[...].astype(_BF16)) + b_ref[...]


def _adaln_mod(c, w_mod, b_mod):
    depth, d, n = w_mod.shape
    bsz = c.shape[0]
    return pl.pallas_call(
        _mod_kernel,
        grid=(depth, n // MOD_TILE),
        in_specs=[
            pl.BlockSpec((bsz, d), lambda l, j: (0, 0)),
            pl.BlockSpec((None, d, MOD_TILE), lambda l, j: (l, 0, j)),
            pl.BlockSpec((None, 1, MOD_TILE), lambda l, j: (l, 0, j)),
        ],
        out_specs=pl.BlockSpec((None, bsz, MOD_TILE), lambda l, j: (l, 0, j)),
        out_shape=jax.ShapeDtypeStruct((depth, bsz, n), _F32),
        compiler_params=pltpu.CompilerParams(
            dimension_semantics=("arbitrary", "arbitrary"),
            vmem_limit_bytes=VMEM_LIMIT_BYTES),
        name="adaln_mod",
    )(c, w_mod, b_mod.reshape(depth, 1, n))


def _ffn_kernel(x_ref, mod_ref, g_ref, win_hbm, wout_hbm, *rest, layer, which, d_ff, final):
    if final:
        fg_ref, o_ref, win_ref, wout_ref, stg_in_ref, stg_out_ref, sem = rest
    else:
        o_ref, win_ref, wout_ref, stg_in_ref, stg_out_ref, sem = rest
    chunks = list(range(0, d_ff, FF_CHUNK))

    def gate_cols(c0):
        return slice(c0, c0 + FF_CHUNK)

    def up_cols(c0):
        return slice(d_ff + c0, d_ff + c0 + FF_CHUNK)

    def body(stager):
        x = x_ref[...]
        mod = mod_ref[...]
        h = _modulated_norm(x, g_ref[...], mod).astype(_BF16)
        acc = jnp.zeros(x.shape, _F32)
        for k, c0 in enumerate(chunks):
            if stager is not None:
                stager.use(k)
            g = _dot(h, win_ref[:, gate_cols(c0)])
            u = _dot(h, win_ref[:, up_cols(c0)])
            a = (g * jax.nn.sigmoid(g) * u).astype(_BF16)
            acc = acc + _dot(a, wout_ref[c0:c0 + FF_CHUNK, :])
        y = x + (0.5 * mod[2:3]) * acc
        if final:
            y = y * lax.rsqrt(jnp.mean(y * y, axis=-1, keepdims=True) + EPS) * fg_ref[...]
        o_ref[...] = y

    pieces = [[
        (win_hbm.at[layer, which, :, gate_cols(c0)], lambda s: stg_in_ref.at[s, 0], win_ref.at[:, gate_cols(c0)]),
        (win_hbm.at[layer, which, :, up_cols(c0)], lambda s: stg_in_ref.at[s, 1], win_ref.at[:, up_cols(c0)]),
        (wout_hbm.at[layer, which, c0:c0 + FF_CHUNK, :], lambda s: stg_out_ref.at[s], wout_ref.at[c0:c0 + FF_CHUNK, :]),
    ] for c0 in chunks]

    first = pl.program_id(0) == 0

    @pl.when(first)
    def _():
        body(_WeightStager(pieces, sem))

    @pl.when(jnp.logical_not(first))
    def _():
        body(None)


def _ffn(x, mod, g, w_in, w_out, layer, which, seq, final_g=None):
    m, d = x.shape
    d_ff = w_out.shape[2]
    final = final_g is not None
    in_specs = [
        pl.BlockSpec((FFN_TILE, d), lambda i: (i, 0)),
        _mod_spec(layer, 2 * which, seq // FFN_TILE, d),
        _resident((1, d)),
        pl.BlockSpec(memory_space=pl.ANY),
        pl.BlockSpec(memory_space=pl.ANY),
    ]
    args = [x, mod, g.reshape(1, d), w_in, w_out]
    if final:
        in_specs.append(_resident((1, d)))
        args.append(final_g.reshape(1, d))
    return pl.pallas_call(
        functools.partial(_ffn_kernel, layer=layer, which=which, d_ff=d_ff, final=final),
        grid=(m // FFN_TILE,),
        in_specs=in_specs,
        out_specs=pl.BlockSpec((FFN_TILE, d), lambda i: (i, 0)),
        out_shape=jax.ShapeDtypeStruct((m, d), _F32),
        scratch_shapes=[
            pltpu.VMEM((d, 2 * d_ff), _BF16),
            pltpu.VMEM((d_ff, d), _BF16),
            pltpu.VMEM((2, 2, d, FF_CHUNK), _F32),
            pltpu.VMEM((2, FF_CHUNK, d), _F32),
            pltpu.SemaphoreType.DMA((2, 3)),
        ],
        compiler_params=pltpu.CompilerParams(
            dimension_semantics=("arbitrary",),
            vmem_limit_bytes=VMEM_LIMIT_BYTES),
        name="ffn_final" if final else "ffn",
    )(*args)


def _mixer_ab_kernel(x_ref, mod_ref, g_ref, win_hbm, nv_ref, ws_ref, bias_ref, cw_ref, wout_hbm,
                     o_ref, win_ref, wout_ref, stg_ref, sem, cx_ref, y_ref, *, layer_slot, tiles_per_batch):
    i = pl.program_id(0)
    tm, d = x_ref.shape
    n_in = win_ref.shape[1] // AB_PIECE

    def in_cols(k):
        return slice(k * AB_PIECE, (k + 1) * AB_PIECE)

    seg_order = (1, 0, 3, 4, 2)

    def body(stager):
        def w_in(seg):
            if stager is not None:
                stager.use(seg_order.index(seg))
            return win_ref[:, in_cols(seg)]

        @pl.when(i % tiles_per_batch == 0)
        def _():
            cx_ref[0:CONV_HALO, :] = jnp.zeros((CONV_HALO, D_B), _F32)

        x = x_ref[...]
        mod = mod_ref[...]
        h = _modulated_norm(x, g_ref[...], mod).astype(_BF16)

        v = _dot(h, w_in(1))
        u = _dot(h, w_in(0))
        cg = _dot(h, w_in(3))
        xb = _dot(h, w_in(4))
        bg = _dot(h, w_in(2))

        v = _gelu_tanh(v)
        u = _gelu_tanh(u)
        mu = jnp.mean(v, axis=-1, keepdims=True)
        vc = v - mu
        var = jnp.mean(vc * vc, axis=-1, keepdims=True)
        vn = ((vc * lax.rsqrt(var + EPS)) * nv_ref[...]).astype(_BF16)
        row = lax.broadcasted_iota(jnp.int32, (CHUNK, CHUNK), 0)
        col = lax.broadcasted_iota(jnp.int32, (CHUNK, CHUNK), 1)
        causal = col <= row
        bias = bias_ref[...]
        for hd in range(A_HEADS):
            lanes = slice(hd * A_HEAD_DIM, (hd + 1) * A_HEAD_DIM)
            w_hd = jnp.where(causal, ws_ref[hd], 0.0).astype(_BF16)
            for ch in range(tm // CHUNK):
                rows = slice(ch * CHUNK, (ch + 1) * CHUNK)
                z = _dot(w_hd, vn[rows, lanes]) + bias[:, lanes]
                y_ref[rows, lanes] = (u[rows, lanes] * z).astype(_BF16)

        cx = cg * xb
        cx_ref[CONV_HALO:CONV_HALO + tm, :] = cx
        cw = cw_ref[...]
        conv = cw[0:1] * cx_ref[CONV_HALO - 2:CONV_HALO - 2 + tm, :]
        conv = conv + cw[1:2] * cx_ref[CONV_HALO - 1:CONV_HALO - 1 + tm, :]
        conv = conv + cw[2:3] * cx
        cx_ref[0:CONV_HALO, :] = cx_ref[tm:tm + CONV_HALO, :]
        y_ref[:, D_A:D_A + D_B] = (bg * conv).astype(_BF16)

        if stager is not None:
            for k in range(n_in, len(stager.pieces)):
                stager.use(k)
        y = _dot(y_ref[:, 0:D_A], wout_ref[0:D_A, :]) + _dot(y_ref[:, D_A:D_A + D_B], wout_ref[D_A:D_A + D_B, :])
        o_ref[...] = x + mod[2:3] * y

    def stage(s):
        return stg_ref.at[s]

    assert sorted(seg_order) == list(range(n_in))
    pieces = [[(win_hbm.at[layer_slot, :, in_cols(k)], stage, win_ref.at[:, in_cols(k)])] for k in seg_order]
    pieces += [[(wout_hbm.at[layer_slot, :, in_cols(k)], stage, wout_ref.at[:, in_cols(k)])]
               for k in range(d // AB_PIECE)]

    @pl.when(i == 0)
    def _():
        body(_WeightStager(pieces, sem))

    @pl.when(i != 0)
    def _():
        body(None)


def _mixer_ab(x, mod, g, w_in, norm_v, w_s, b_s, conv_w, w_out, layer, layer_slot, seq):
    m, d = x.shape
    d_mix = w_out.shape[1]
    tiles_per_batch = seq // MIX_TILE
    bias = jnp.repeat(b_s.T, A_HEAD_DIM, axis=1)
    return pl.pallas_call(
        functools.partial(_mixer_ab_kernel, layer_slot=layer_slot, tiles_per_batch=tiles_per_batch),
        grid=(m // MIX_TILE,),
        in_specs=[
            pl.BlockSpec((MIX_TILE, d), lambda i: (i, 0)),
            _mod_spec(layer, 1, tiles_per_batch, d),
            _resident((1, d)),
            pl.BlockSpec(memory_space=pl.ANY),
            _resident((1, D_A)),
            _resident(w_s.shape),
            _resident(bias.shape),
            _resident(conv_w.shape),
            pl.BlockSpec(memory_space=pl.ANY),
        ],
        out_specs=pl.BlockSpec((MIX_TILE, d), lambda i: (i, 0)),
        out_shape=jax.ShapeDtypeStruct((m, d), _F32),
        scratch_shapes=[
            pltpu.VMEM((d, w_in.shape[2]), _BF16),
            pltpu.VMEM((d_mix, d), _BF16),
            pltpu.VMEM((2, d, AB_PIECE), _F32),
            pltpu.SemaphoreType.DMA((2, 1)),
            pltpu.VMEM((CONV_HALO + MIX_TILE, D_B), _F32),
            pltpu.VMEM((MIX_TILE, D_A + D_B), _BF16),
        ],
        compiler_params=pltpu.CompilerParams(
            dimension_semantics=("arbitrary",),
            vmem_limit_bytes=VMEM_LIMIT_BYTES),
        name="mixer_ab",
    )(x, mod, g.reshape(1, d), w_in, norm_v.reshape(1, D_A), w_s, bias, conv_w, w_out)


def _mixer_pool_kernel(x_ref, mod_ref, g_ref, wg_ref, sc_ref, o_ref, h_ref, *, tiles_per_batch):
    i = pl.program_id(0)
    tm = x_ref.shape[0]
    gd = wg_ref.shape[-1]

    @pl.when(i % tiles_per_batch == 0)
    def _():
        h_ref[0:POOL_HALO, :] = jnp.zeros((POOL_HALO, h_ref.shape[1]), _F32)

    x = x_ref[...]
    mod = mod_ref[...]
    h = _modulated_norm(x, g_ref[...], mod)
    h_ref[POOL_HALO:POOL_HALO + tm, :] = h

    h_ext = h_ref[...]
    h_hi = h_ext.astype(_BF16)
    h_lo = (h_ext - h_hi.astype(_F32)).astype(_BF16)
    t_idx = lax.broadcasted_iota(jnp.int32, (POOL_BLOCK, POOL_BLOCK + POOL_HALO), 0)
    s_idx = lax.broadcasted_iota(jnp.int32, (POOL_BLOCK, POOL_BLOCK + POOL_HALO), 1) - POOL_HALO
    pos = (i % tiles_per_batch) * tm + lax.broadcasted_iota(jnp.int32, (tm, 1), 0) + 1
    scale = sc_ref[...]
    for k, w in enumerate(POOL_WINDOWS):
        lanes = slice(k * gd, (k + 1) * gd)
        band = jnp.where((s_idx <= t_idx) & (s_idx > t_idx - w), 1.0, 0.0).astype(_BF16)
        wsum = []
        for r0 in range(0, tm, POOL_BLOCK):
            rows_ext = slice(r0, r0 + POOL_BLOCK + POOL_HALO)
            wsum.append(_dot(band, h_hi[rows_ext, lanes]) + _dot(band, h_lo[rows_ext, lanes]))
        wsum = jnp.concatenate(wsum, axis=0)
        cnt = jnp.minimum(pos, w).astype(_F32)
        p = wsum / cnt - h[:, lanes]
        y = _dot(p.astype(_BF16), wg_ref[k].astype(_BF16)) * scale[:, lanes]
        o_ref[:, lanes] = x[:, lanes] + mod[2:3, lanes] * y
    h_ref[0:POOL_HALO, :] = h_ref[tm:tm + POOL_HALO, :]


def _mixer_pool(x, mod, g, w_grp, scale, layer, seq):
    m, d = x.shape
    tiles_per_batch = seq // MIX_TILE
    return pl.pallas_call(
        functools.partial(_mixer_pool_kernel, tiles_per_batch=tiles_per_batch),
        grid=(m // MIX_TILE,),
        in_specs=[
            pl.BlockSpec((MIX_TILE, d), lambda i: (i, 0)),
            _mod_spec(layer, 1, tiles_per_batch, d),
            _resident((1, d)),
            _resident(w_grp.shape),
            _resident((1, d)),
        ],
        out_specs=pl.BlockSpec((MIX_TILE, d), lambda i: (i, 0)),
        out_shape=jax.ShapeDtypeStruct((m, d), _F32),
        scratch_shapes=[pltpu.VMEM((POOL_HALO + MIX_TILE, d), _F32)],
        compiler_params=pltpu.CompilerParams(
            dimension_semantics=("arbitrary",),
            vmem_limit_bytes=VMEM_LIMIT_BYTES),
        name="mixer_pool",
    )(x, mod, g.reshape(1, d), w_grp, scale.reshape(1, d))


def kernel(x, c, norm_g, w_mod, b_mod, w_ffn_in, w_ffn_out, ab_w_in, ab_norm_v, ab_w_s, ab_b_s,
           ab_conv_w, ab_w_out, pool_w_grp, pool_scale, final_g):
    bsz, seq, d = x.shape
    depth = norm_g.shape[0]
    assert seq % FFN_TILE == 0 and seq % MIX_TILE == 0 and MIX_TILE % CHUNK == 0
    assert w_ffn_out.shape[2] % FF_CHUNK == 0 and w_mod.shape[2] % MOD_TILE == 0
    assert ab_w_in.shape[2] % AB_PIECE == 0 and d % AB_PIECE == 0

    mod = _adaln_mod(c, w_mod, b_mod).reshape(depth, bsz, 3, 3, d)

    xf = x.reshape(bsz * seq, d)
    for l in range(depth):
        j = l // 2
        xf = _ffn(xf, mod, norm_g[l, 0], w_ffn_in, w_ffn_out, l, 0, seq)
        if l % 2 == 0:
            xf = _mixer_ab(xf, mod, norm_g[l, 1], ab_w_in, ab_norm_v[j], ab_w_s[j], ab_b_s[j], ab_conv_w[j],
                           ab_w_out, l, j, seq)
        else:
            xf = _mixer_pool(xf, mod, norm_g[l, 1], pool_w_grp[j], pool_scale[j], l, seq)
        last = l == depth - 1
        xf = _ffn(xf, mod, norm_g[l, 2], w_ffn_in, w_ffn_out, l, 1, seq,
                  final_g=final_g if last else None)
    return xf.reshape(bsz, seq, d)
```
